```python
import jax, jax.numpy as jnp
from jax import lax
import numpy as np

D_MODEL = 1024
BATCH = 16
SEQ = 4096
DEPTH = 1
DEC_BATCH = 32
DEC_SEQ = 64
PAST_LEN = 4096

CHUNK = 64
D_CONV_A = D_MODEL
D_CONV_B = D_MODEL
CONV_A_WIDTH = 31
CONV_B_WIDTH = 3
D_FF = 4 * D_MODEL
N_IN = 2 * D_CONV_A + 3 * D_CONV_B + 2 * D_MODEL
EPS = 1e-6

kernel_name = "parallel_conformer_shortconv_encoder_step"


def rmsnorm(x, g):
    xf = x.astype(jnp.float32)
    y = xf * lax.rsqrt(jnp.mean(xf * xf, axis=-1, keepdims=True) + EPS)
    return (y * g.astype(jnp.float32)).astype(x.dtype)


def layernorm(x, g, b):
    xf = x.astype(jnp.float32)
    mu = jnp.mean(xf, axis=-1, keepdims=True)
    var = jnp.mean(jnp.square(xf - mu), axis=-1, keepdims=True)
    y = (xf - mu) * lax.rsqrt(var + EPS)
    return (y * g.astype(jnp.float32) + b.astype(jnp.float32)).astype(x.dtype)


def causal_dwconv(x, hist, w):
    width, ch = w.shape
    xx = jnp.concatenate([hist.astype(x.dtype), x], axis=1)
    out = lax.conv_general_dilated(
        xx, w.astype(x.dtype)[:, None, :], window_strides=(1,), padding='VALID',
        dimension_numbers=('NWC', 'WIO', 'NWC'), feature_group_count=ch)
    return out, xx[:, xx.shape[1] - (width - 1):]


def layer(x, hist_a, hist_b, norm1_g, w_in, dw_a, dw_a_b, ln_a_g, ln_a_b, wa_out,
          dw_b, wb_out, gate_bias, w_o, norm2_g, w_up, w_down):
    h = rmsnorm(x, norm1_g)
    proj = jnp.einsum('btd,dn->btn', h, w_in)
    o1 = D_CONV_A
    o2 = o1 + D_CONV_A
    o3 = o2 + D_CONV_B
    o4 = o3 + D_CONV_B
    o5 = o4 + D_CONV_B
    o6 = o5 + D_MODEL
    a_val, a_gate = proj[..., :o1], proj[..., o1:o2]
    g_b, g_c, h_b = proj[..., o2:o3], proj[..., o3:o4], proj[..., o4:o5]
    gate_a = jax.nn.sigmoid(proj[..., o5:o6] + gate_bias[:D_MODEL])
    gate_b = jax.nn.sigmoid(proj[..., o6:] + gate_bias[D_MODEL:])
    u = a_val * jax.nn.sigmoid(a_gate)
    ca, new_a = causal_dwconv(u, hist_a, dw_a)
    za = jax.nn.silu(layernorm(ca + dw_a_b, ln_a_g, ln_a_b))
    y_a = jnp.einsum('btc,cd->btd', za, wa_out)
    cb, new_b = causal_dwconv(g_c * h_b, hist_b, dw_b)
    y_b = jnp.einsum('btc,cd->btd', g_b * cb, wb_out)
    x = x + jnp.einsum('btd,de->bte', gate_a * y_a + gate_b * y_b, w_o)
    h2 = rmsnorm(x, norm2_g)
    hid = jnp.square(jax.nn.relu(jnp.einsum('btd,df->btf', h2, w_up)))
    x = x + jnp.einsum('btf,fd->btd', hid, w_down)
    return x, new_a, new_b


def setup_inputs(seed: int = 0) -> dict:
    key = jax.random.key(seed)
    ks = jax.random.split(key, 20)
    f32 = jnp.float32
    nrm = lambda k, s, sc: jax.random.normal(k, s, f32) * sc
    L = DEPTH
    return {
        "x_prompt": nrm(ks[0], (BATCH, SEQ, D_MODEL), 1.0),
        "x_sample": nrm(ks[1], (DEC_BATCH, DEC_SEQ, D_MODEL), 1.0),
        "cache_conv_a": nrm(ks[2], (L, DEC_BATCH, CONV_A_WIDTH - 1, D_CONV_A), 0.5),
        "cache_conv_b": nrm(ks[3], (L, DEC_BATCH, CONV_B_WIDTH - 1, D_CONV_B), 0.5),
        "norm1_g": 1.0 + nrm(ks[4], (L, D_MODEL), 0.01),
        "w_in": nrm(ks[5], (L, D_MODEL, N_IN), D_MODEL ** -0.5),
        "dw_a": nrm(ks[6], (L, CONV_A_WIDTH, D_CONV_A), CONV_A_WIDTH ** -0.5),
        "dw_a_b": nrm(ks[7], (L, D_CONV_A), 0.01),
        "ln_a_g": 1.0 + nrm(ks[8], (L, D_CONV_A), 0.01),
        "ln_a_b": nrm(ks[9], (L, D_CONV_A), 0.01),
        "wa_out": nrm(ks[10], (L, D_CONV_A, D_MODEL), D_CONV_A ** -0.5),
        "dw_b": nrm(ks[11], (L, CONV_B_WIDTH, D_CONV_B), CONV_B_WIDTH ** -0.5),
        "wb_out": nrm(ks[12], (L, D_CONV_B, D_MODEL), D_CONV_B ** -0.5),
        "gate_bias": nrm(ks[13], (L, 2 * D_MODEL), 0.01),
        "w_o": nrm(ks[14], (L, D_MODEL, D_MODEL), D_MODEL ** -0.5),
        "norm2_g": 1.0 + nrm(ks[15], (L, D_MODEL), 0.01),
        "w_up": nrm(ks[16], (L, D_MODEL, D_FF), D_MODEL ** -0.5),
        "w_down": nrm(ks[17], (L, D_FF, D_MODEL), D_FF ** -0.5),
        "final_norm_g": 1.0 + nrm(ks[18], (D_MODEL,), 0.01),
    }


def reference(x_prompt, x_sample, cache_conv_a, cache_conv_b, norm1_g, w_in, dw_a, dw_a_b,
              ln_a_g, ln_a_b, wa_out, dw_b, wb_out, gate_bias, w_o, norm2_g, w_up, w_down,
              final_norm_g):
    xp, xs = x_prompt, x_sample
    pa, pb, sa, sb = [], [], [], []
    for l in range(DEPTH):
        params = (norm1_g[l], w_in[l], dw_a[l], dw_a_b[l], ln_a_g[l], ln_a_b[l], wa_out[l],
                  dw_b[l], wb_out[l], gate_bias[l], w_o[l], norm2_g[l], w_up[l], w_down[l])
        zero_a = jnp.zeros((xp.shape[0], CONV_A_WIDTH - 1, D_CONV_A), xp.dtype)
        zero_b = jnp.zeros((xp.shape[0], CONV_B_WIDTH - 1, D_CONV_B), xp.dtype)
        xp, na, nb = layer(xp, zero_a, zero_b, *params)
        pa.append(na)
        pb.append(nb)
        xs, na, nb = layer(xs, cache_conv_a[l], cache_conv_b[l], *params)
        sa.append(na)
        sb.append(nb)
    y_prompt = rmsnorm(xp, final_norm_g)
    y_sample = rmsnorm(xs, final_norm_g)
    prompt_conv_a = jnp.stack(pa, axis=0)
    prompt_conv_b = jnp.stack(pb, axis=0)
    sample_conv_a = jnp.stack(sa, axis=0)
    sample_conv_b = jnp.stack(sb, axis=0)
    return (y_prompt, y_sample, prompt_conv_a, prompt_conv_b, sample_conv_a, sample_conv_b)
```

```python
import functools

import jax
import jax.numpy as jnp
from jax import lax
from jax.experimental import pallas as pl
from jax.experimental.pallas import tpu as pltpu

EPS = 1e-6
CONV_A_WIDTH = 31
CONV_B_WIDTH = 3
HALO_A = 32
HALO_B = 8
SUBLANES = 8
VMEM_LIMIT_BYTES = 56 * 1024 * 1024


def _rmsnorm(x, g):
    ms = jnp.mean(x * x, axis=-1, keepdims=True)
    return x * lax.rsqrt(ms + EPS) * g


def _sigmoid(x):
    return 1.0 / (1.0 + jnp.exp(-x))


def _dot(a, b):
    return jnp.dot(a, b, preferred_element_type=jnp.float32)


def _causal_dwconv(scr, w_ref, halo, width, nb, tt, row_chunk):
    base = halo - (width - 1)
    w = [w_ref[k:k + 1, :] for k in range(width)]
    rows = []
    for b in range(nb):
        for r0 in range(0, tt, row_chunk):
            acc = w[0] * scr[b, base + r0:base + r0 + row_chunk, :]
            for k in range(1, width):
                acc = acc + w[k] * scr[b, base + r0 + k:base + r0 + k + row_chunk, :]
            rows.append(acc)
    return jnp.concatenate(rows, axis=0)


def _mixer_kernel(*refs, nb, tt, has_hist):
    if has_hist:
        x_ref, hista_ref, histb_ref = refs[:3]
        refs = refs[3:]
    else:
        x_ref = refs[0]
        refs = refs[1:]
    (n1g_ref, win_ref, dwa_ref, dwab_ref, lng_ref, lnb_ref, wa_ref, dwb_ref, wb_ref,
     gbias_ref, wo_ref, x1_ref, sa_ref, sb_ref, ua_scr, vb_scr) = refs
    d = x_ref.shape[-1]
    m = nb * tt
    t = pl.program_id(1)

    @pl.when(t == 0)
    def _init_history():
        if has_hist:
            ua_scr[:, :HALO_A, :] = hista_ref[...]
            vb_scr[:, :HALO_B, :] = histb_ref[...]
        else:
            ua_scr[:, :HALO_A, :] = jnp.zeros((nb, HALO_A, d), jnp.float32)
            vb_scr[:, :HALO_B, :] = jnp.zeros((nb, HALO_B, d), jnp.float32)

    x = x_ref[...].reshape(m, d)
    h = _rmsnorm(x, n1g_ref[...]).astype(jnp.bfloat16)

    def proj(j):
        return _dot(h, win_ref[:, j * d:(j + 1) * d])

    u = proj(0) * _sigmoid(proj(1))
    ua_scr[:, HALO_A:, :] = u.reshape(nb, tt, d)
    ca = _causal_dwconv(ua_scr, dwa_ref, HALO_A, CONV_A_WIDTH, nb, tt, SUBLANES) + dwab_ref[...]
    mu = jnp.mean(ca, axis=-1, keepdims=True)
    cc = ca - mu
    var = jnp.mean(cc * cc, axis=-1, keepdims=True)
    ln = cc * lax.rsqrt(var + EPS) * lng_ref[...] + lnb_ref[...]
    za = ln * _sigmoid(ln)
    y_a = _dot(za.astype(jnp.bfloat16), wa_ref[...])

    g_b = proj(2)
    v = proj(3) * proj(4)
    vb_scr[:, HALO_B:, :] = v.reshape(nb, tt, d)
    cb = _causal_dwconv(vb_scr, dwb_ref, HALO_B, CONV_B_WIDTH, nb, tt, SUBLANES)
    y_b = _dot((g_b * cb).astype(jnp.bfloat16), wb_ref[...])

    gate_a = _sigmoid(proj(5) + gbias_ref[:, :d])
    gate_b = _sigmoid(proj(6) + gbias_ref[:, d:])
    merged = (gate_a * y_a + gate_b * y_b).astype(jnp.bfloat16)
    x1_ref[...] = (x + _dot(merged, wo_ref[...])).reshape(nb, tt, d)

    tail_a = ua_scr[:, tt:tt + HALO_A, :]
    tail_b = vb_scr[:, tt:tt + HALO_B, :]
    ua_scr[:, :HALO_A, :] = tail_a
    vb_scr[:, :HALO_B, :] = tail_b
    sa_ref[...] = tail_a
    sb_ref[...] = tail_b


def _mlp_kernel(*refs, ff_chunk, final_norm):
    if final_norm:
        x1_ref, n2g_ref, wup_ref, wdn_ref, fng_ref, y_ref = refs
    else:
        x1_ref, n2g_ref, wup_ref, wdn_ref, y_ref = refs
    x1 = x1_ref[...]
    h2 = _rmsnorm(x1, n2g_ref[...]).astype(jnp.bfloat16)
    acc = x1
    d_ff = wup_ref.shape[1]
    for c in range(0, d_ff, ff_chunk):
        hid = jnp.maximum(_dot(h2, wup_ref[:, c:c + ff_chunk]), 0.0)
        acc = acc + _dot((hid * hid).astype(jnp.bfloat16), wdn_ref[c:c + ff_chunk, :])
    y_ref[...] = _rmsnorm(acc, fng_ref[...]) if final_norm else acc


def _resident(shape):
    zeros = (0,) * len(shape)
    return pl.BlockSpec(shape, lambda *_: zeros, pipeline_mode=pl.Buffered(1))


def _mixer(x, hist_a, hist_b, params, *, nb, tt):
    batch, seq, d = x.shape
    assert batch % nb == 0 and seq % tt == 0 and tt % SUBLANES == 0
    assert tt >= HALO_A, "the carried history must come from the current tile alone"
    has_hist = hist_a is not None
    grid = (batch // nb, seq // tt)
    tile = pl.BlockSpec((nb, tt, d), lambda b, t: (b, t, 0))
    per_batch_a = pl.BlockSpec((nb, HALO_A, d), lambda b, t: (b, 0, 0))
    per_batch_b = pl.BlockSpec((nb, HALO_B, d), lambda b, t: (b, 0, 0))
    in_specs = [tile]
    args = [x]
    if has_hist:
        in_specs += [per_batch_a, per_batch_b]
        args += [hist_a, hist_b]
    in_specs += [_resident(p.shape) for p in params]
    args += list(params)
    return pl.pallas_call(
        functools.partial(_mixer_kernel, nb=nb, tt=tt, has_hist=has_hist),
        grid=grid,
        in_specs=in_specs,
        out_specs=[tile, per_batch_a, per_batch_b],
        out_shape=[
            jax.ShapeDtypeStruct((batch, seq, d), jnp.float32),
            jax.ShapeDtypeStruct((batch, HALO_A, d), jnp.float32),
            jax.ShapeDtypeStruct((batch, HALO_B, d), jnp.float32),
        ],
        scratch_shapes=[
            pltpu.VMEM((nb, HALO_A + tt, d), jnp.float32),
            pltpu.VMEM((nb, HALO_B + tt, d), jnp.float32),
        ],
        compiler_params=pltpu.CompilerParams(
            dimension_semantics=("arbitrary", "arbitrary"),
            vmem_limit_bytes=VMEM_LIMIT_BYTES,
        ),
        name="mixer_hist" if has_hist else "mixer",
    )(*args)


def _mlp(x1, params, *, tm, ff_chunk, final_norm):
    n, d = x1.shape
    assert n % tm == 0
    tile = pl.BlockSpec((tm, d), lambda i: (i, 0))
    return pl.pallas_call(
        functools.partial(_mlp_kernel, ff_chunk=ff_chunk, final_norm=final_norm),
        grid=(n // tm,),
        in_specs=[tile] + [_resident(p.shape) for p in params],
        out_specs=tile,
        out_shape=jax.ShapeDtypeStruct((n, d), jnp.float32),
        compiler_params=pltpu.CompilerParams(
            dimension_semantics=("arbitrary",),
            vmem_limit_bytes=VMEM_LIMIT_BYTES,
        ),
        name="mlp",
    )(x1, *params)


def _pad_history(cache, halo):
    return jnp.pad(cache, ((0, 0), (halo - cache.shape[1], 0), (0, 0)))


def kernel(x_prompt, x_sample, cache_conv_a, cache_conv_b, norm1_g, w_in, dw_a, dw_a_b, ln_a_g, ln_a_b, wa_out, dw_b, wb_out, gate_bias, w_o, norm2_g, w_up, w_down, final_norm_g):
    depth = w_in.shape[0]
    d = x_prompt.shape[-1]
    bf16 = jnp.bfloat16
    row = lambda a: a.reshape(1, -1)
    xp, xs = x_prompt, x_sample
    pa, pb, sa, sb = [], [], [], []
    for l in range(depth):
        mixer_params = (row(norm1_g[l]), w_in[l].astype(bf16), dw_a[l], row(dw_a_b[l]),
                        row(ln_a_g[l]), row(ln_a_b[l]), wa_out[l].astype(bf16), dw_b[l],
                        wb_out[l].astype(bf16), row(gate_bias[l]), w_o[l].astype(bf16))
        last = l == depth - 1
        mlp_params = (row(norm2_g[l]), w_up[l].astype(bf16), w_down[l].astype(bf16))
        if last:
            mlp_params = mlp_params + (row(final_norm_g),)
        mlp = functools.partial(_mlp, tm=256, ff_chunk=1024, final_norm=last)

        xp1, na, nb_ = _mixer(xp, None, None, mixer_params, nb=1, tt=256)
        pa.append(na[:, HALO_A - (CONV_A_WIDTH - 1):])
        pb.append(nb_[:, HALO_B - (CONV_B_WIDTH - 1):])
        xp = mlp(xp1.reshape(-1, d), mlp_params).reshape(xp.shape)

        xs1, na, nb_ = _mixer(xs, _pad_history(cache_conv_a[l], HALO_A),
                              _pad_history(cache_conv_b[l], HALO_B), mixer_params, nb=4, tt=64)
        sa.append(na[:, HALO_A - (CONV_A_WIDTH - 1):])
        sb.append(nb_[:, HALO_B - (CONV_B_WIDTH - 1):])
        xs = mlp(xs1.reshape(-1, d), mlp_params).reshape(xs.shape)
    return (xp, xs, jnp.stack(pa, axis=0), jnp.stack(pb, axis=0),
            jnp.stack(sa, axis=0), jnp.stack(sb, axis=0))
```
